```python
import math
import jax
import jax.numpy as jnp
from jax import lax
import numpy as np

D_MODEL = 1024
BATCH = 8
SEQ = 2048
DEPTH = 1
DEC_BATCH = 128
DEC_SEQ = 4
PAST_LEN = 8192
PAGE_SIZE = 128

N_Q_HEADS = 8
N_KV_HEADS = 2
Q_PER_KV = N_Q_HEADS // N_KV_HEADS
HEAD_DIM = 64
ATTN_WIDTH = N_Q_HEADS * HEAD_DIM
KV_WIDTH = N_KV_HEADS * HEAD_DIM
CMP_BLOCK = 32
CMP_STRIDE = 16
CMP_RATIO = CMP_BLOCK // CMP_STRIDE
SLC_BLOCK = 64
N_SELECT = 16
WINDOW = 512
WIN_Q_BLOCK = 128
SLC_Q_BLOCK = 64
SSM_WIDTH = D_MODEL // 2
SSM_GROUP = 16
N_SSM_GROUPS = SSM_WIDTH // SSM_GROUP
SSM_STATE = 64
DT_MIN = 1e-3
DT_MAX = 1e-1
N_BUCKETS = 32
MAX_DISTANCE = 128
D_FF = 4 * D_MODEL
IN_SPLITS = (ATTN_WIDTH,) + (KV_WIDTH,) * 6 + (3 * N_Q_HEADS, SSM_WIDTH, D_MODEL, D_MODEL)
IN_WIDTH = sum(IN_SPLITS)
RMS_EPS = 1e-6
NEG_INF = -1e30
TINY = 1e-30

kernel_name = 'nsa_s5_parallel_hybrid_step'


def rms_norm(x, g):
    xf = x.astype(jnp.float32)
    y = xf * lax.rsqrt(jnp.mean(xf * xf, axis=-1, keepdims=True) + RMS_EPS)
    return (y * g.astype(jnp.float32)).astype(x.dtype)


def t5_bucket(dist):
    n = jnp.maximum(dist, 0)
    exact = N_BUCKETS // 2
    nf = jnp.maximum(n, exact).astype(jnp.float32)
    large = exact + (jnp.log(nf / exact) / math.log(MAX_DISTANCE / exact) * (N_BUCKETS - exact)).astype(jnp.int32)
    return jnp.where(n < exact, n, jnp.minimum(large, N_BUCKETS - 1))


def shared_bias(table, dist):
    b = jnp.moveaxis(table[t5_bucket(dist)].astype(jnp.float32), -1, 0)
    return b.reshape((N_KV_HEADS, Q_PER_KV) + dist.shape)


def gathered_bias(table, dist):
    table_g = table.T.reshape(N_KV_HEADS, Q_PER_KV, N_BUCKETS)
    g_idx = jnp.arange(N_KV_HEADS)[None, :, None, None]
    b = table_g[g_idx, :, t5_bucket(dist)]
    return jnp.moveaxis(b, -1, 2).astype(jnp.float32)


def masked_softmax(s, mask):
    s = jnp.where(mask, s, NEG_INF)
    p = jnp.where(mask, jnp.exp(s - jnp.max(s, axis=-1, keepdims=True)), 0.0)
    return p / jnp.maximum(jnp.sum(p, axis=-1, keepdims=True), TINY)


def attend_shared(q, k, v, dist, mask, table):
    s = jnp.einsum('ngrtd,nsgd->ngrts', q, k).astype(jnp.float32) + shared_bias(table, dist)
    p = masked_softmax(s, mask)
    return jnp.einsum('ngrts,nsgd->ngrtd', p.astype(v.dtype), v), p


def slc_attend(q, kg, vg, kpos, kok, q_pos, table):
    dist = q_pos[None, None, :, None] - kpos
    mask = kok & (dist >= 0)
    s = jnp.einsum('ngrtd,ngtsd->ngrts', q, kg).astype(jnp.float32) + gathered_bias(table, dist)
    p = masked_softmax(s, mask[:, :, None])
    return jnp.einsum('ngrts,ngtsd->ngrtd', p.astype(vg.dtype), vg)


def compress(x, pe, w1, b1, w2):
    n, length, g, hd = x.shape
    n_chunks = -(-length // CMP_STRIDE)
    x = jnp.pad(x, ((0, 0), (0, n_chunks * CMP_STRIDE - length), (0, 0), (0, 0)))
    x = x.reshape(n, n_chunks, CMP_STRIDE, g, hd)
    w1c = w1.reshape(CMP_RATIO, CMP_STRIDE, hd, hd)
    pec = pe.reshape(CMP_RATIO, CMP_STRIDE, hd)
    part = jnp.einsum('ncsgd,jsde->jncge', x, w1c)
    n_cmp = n_chunks - CMP_RATIO + 1
    hid = b1 + jnp.einsum('jsd,jsde->e', pec, w1c)
    for j in range(CMP_RATIO):
        hid = hid + part[j, :, j:j + n_cmp]
    return jnp.einsum('ncge,ef->ncgf', jax.nn.silu(hid), w2)


def cmp_ends(n_cmp):
    return jnp.arange(n_cmp) * CMP_STRIDE + CMP_BLOCK - 1


def select_blocks(imp, q_pos, n_slc):
    n_cmp = imp.shape[-1]
    chunk = jnp.arange(n_cmp)[:, None] + jnp.arange(CMP_RATIO)[None, :]
    chunk_blk = chunk // (SLC_BLOCK // CMP_STRIDE)
    blk = jnp.arange(n_slc)
    overlap = jnp.sum(chunk_blk[:, :, None] == blk[None, None, :], axis=1).astype(jnp.float32)
    score = jnp.einsum('ngtc,cj->ngtj', imp.astype(jnp.float32), overlap)
    cur = (q_pos // SLC_BLOCK)[:, None]
    forced = (blk[None, :] == 0) | (blk[None, :] == cur) | (blk[None, :] == cur - 1)
    avail = blk[None, :] * SLC_BLOCK <= q_pos[:, None]
    score = jnp.where(forced, jnp.inf, jnp.where(avail, score, -jnp.inf))
    top, idx = lax.top_k(score, min(N_SELECT, n_slc))
    return idx, top > -jnp.inf


def slc_prompt(q, k, v, idx, ok, table):
    n, g, r, t, hd = q.shape
    n_slc = t // SLC_BLOCK
    kb = k.reshape(n, n_slc, SLC_BLOCK, g, hd).transpose(0, 3, 1, 2, 4)
    vb = v.reshape(n, n_slc, SLC_BLOCK, g, hd).transpose(0, 3, 1, 2, 4)
    nqb = t // SLC_Q_BLOCK
    kk = idx.shape[-1]
    n_ar = jnp.arange(n)[:, None, None, None]
    g_ar = jnp.arange(g)[None, :, None, None]
    s_ar = jnp.arange(SLC_BLOCK)

    def block(args):
        qb, ib, okb, pb = args
        kg = kb[n_ar, g_ar, ib].reshape(n, g, SLC_Q_BLOCK, kk * SLC_BLOCK, hd)
        vg = vb[n_ar, g_ar, ib].reshape(n, g, SLC_Q_BLOCK, kk * SLC_BLOCK, hd)
        kpos = (ib[..., None] * SLC_BLOCK + s_ar).reshape(n, g, SLC_Q_BLOCK, -1)
        kok = jnp.broadcast_to(okb[..., None], ib.shape + (SLC_BLOCK,)).reshape(n, g, SLC_Q_BLOCK, -1)
        return slc_attend(qb, kg, vg, kpos, kok, pb, table)

    xs = (q.reshape(n, g, r, nqb, SLC_Q_BLOCK, hd).transpose(3, 0, 1, 2, 4, 5),
          idx.reshape(n, g, nqb, SLC_Q_BLOCK, kk).transpose(2, 0, 1, 3, 4),
          ok.reshape(n, g, nqb, SLC_Q_BLOCK, kk).transpose(2, 0, 1, 3, 4),
          jnp.arange(t).reshape(nqb, SLC_Q_BLOCK))
    o = lax.map(block, xs)
    return o.transpose(1, 2, 3, 0, 4, 5).reshape(n, g, r, t, hd)


def slc_sample(q, k_new, v_new, idx, ok, q_pos, cache_k, cache_v, layer, page_table, table):
    n, g, r, t, hd = q.shape
    kk = idx.shape[-1]
    past_blk = page_table.shape[1] * PAGE_SIZE // SLC_BLOCK
    pb = PAGE_SIZE // SLC_BLOCK
    n_ar = jnp.arange(n)[:, None, None, None]
    g5 = jnp.arange(g)[None, :, None, None, None]
    s_ar = jnp.arange(SLC_BLOCK)
    jc = jnp.minimum(idx, past_blk - 1)
    phys = page_table[n_ar, jc // pb]
    rows = (jc % pb)[..., None] * SLC_BLOCK + s_ar
    kg = cache_k[layer, phys[..., None], rows, g5].reshape(n, g, t, kk * SLC_BLOCK, hd)
    vg = cache_v[layer, phys[..., None], rows, g5].reshape(n, g, t, kk * SLC_BLOCK, hd)
    kpos_past = (idx[..., None] * SLC_BLOCK + s_ar).reshape(n, g, t, -1)
    kok_past = jnp.broadcast_to((ok & (idx < past_blk))[..., None], idx.shape + (SLC_BLOCK,)).reshape(n, g, t, -1)
    new_blk = q_pos // SLC_BLOCK
    chosen = jnp.any((idx[..., None] == new_blk) & ok[..., None], axis=3)
    k_nb = jnp.broadcast_to(k_new.transpose(0, 2, 1, 3)[:, :, None], (n, g, t, t, hd)).astype(kg.dtype)
    v_nb = jnp.broadcast_to(v_new.transpose(0, 2, 1, 3)[:, :, None], (n, g, t, t, hd)).astype(vg.dtype)
    kg = jnp.concatenate([kg, k_nb], axis=3)
    vg = jnp.concatenate([vg, v_nb], axis=3)
    kpos = jnp.concatenate([kpos_past, jnp.broadcast_to(q_pos, (n, g, t, t))], axis=3)
    kok = jnp.concatenate([kok_past, chosen], axis=3)
    return slc_attend(q, kg, vg, kpos, kok, q_pos, table)


def win_prompt(q, k, v, table):
    n, g, r, t, hd = q.shape
    nb = t // WIN_Q_BLOCK
    nprev = -(-WINDOW // WIN_Q_BLOCK)
    pad = ((0, 0), (nprev * WIN_Q_BLOCK, 0), (0, 0), (0, 0))
    kp = jnp.pad(k, pad).reshape(n, nb + nprev, WIN_Q_BLOCK, g, hd)
    vp = jnp.pad(v, pad).reshape(n, nb + nprev, WIN_Q_BLOCK, g, hd)
    kband = jnp.concatenate([kp[:, j:j + nb] for j in range(nprev + 1)], axis=2)
    vband = jnp.concatenate([vp[:, j:j + nb] for j in range(nprev + 1)], axis=2)
    qpos = jnp.arange(t).reshape(nb, WIN_Q_BLOCK)
    kpos = (jnp.arange(nb)[:, None] - nprev) * WIN_Q_BLOCK + jnp.arange((nprev + 1) * WIN_Q_BLOCK)[None, :]
    dist = qpos[:, :, None] - kpos[:, None, :]
    mask = (dist >= 0) & (dist <= WINDOW) & (kpos >= 0)[:, None, :]
    qb = q.reshape(n, g, r, nb, WIN_Q_BLOCK, hd)
    s = jnp.einsum('ngrbid,nbjgd->ngrbij', qb, kband).astype(jnp.float32) + shared_bias(table, dist)
    p = masked_softmax(s, mask)
    o = jnp.einsum('ngrbij,nbjgd->ngrbid', p.astype(vband.dtype), vband)
    return o.reshape(n, g, r, t, hd)


def win_sample(q, k_new, v_new, buf_k, buf_v, q_pos, past_len, table):
    w_buf = buf_k.shape[1]
    keys = jnp.concatenate([buf_k, k_new.astype(buf_k.dtype)], axis=1)
    vals = jnp.concatenate([buf_v, v_new.astype(buf_v.dtype)], axis=1)
    kpos = past_len - w_buf + jnp.arange(keys.shape[1])
    dist = q_pos[:, None] - kpos[None, :]
    mask = (dist >= 0) & (dist <= WINDOW)
    o, _ = attend_shared(q, keys, vals, dist, mask, table)
    return o, keys[:, -w_buf:], vals[:, -w_buf:]


def s5_ssm(u, x0_re, x0_im, lam_re, lam_im, log_dt, b_re, b_im, c_re, c_im, d, w_glu):
    n, t, _ = u.shape
    f32 = jnp.float32
    uf = u.astype(f32).reshape(n, t, N_SSM_GROUPS, SSM_GROUP)
    dt = jnp.exp(log_dt.astype(f32))[:, None]
    lr = jnp.minimum(lam_re.astype(f32), -1e-4)
    li = lam_im.astype(f32)
    mag = jnp.exp(lr * dt)
    ar, ai = mag * jnp.cos(li * dt), mag * jnp.sin(li * dt)
    den = lr * lr + li * li
    fr = ((ar - 1.0) * lr + ai * li) / den
    fi = (ai * lr - (ar - 1.0) * li) / den
    br, bi = b_re.astype(f32), b_im.astype(f32)
    bbr = fr[..., None] * br - fi[..., None] * bi
    bbi = fr[..., None] * bi + fi[..., None] * br
    bu_r = jnp.einsum('ntgi,gpi->ntgp', uf, bbr)
    bu_i = jnp.einsum('ntgi,gpi->ntgp', uf, bbi)
    x0r, x0i = x0_re.astype(f32), x0_im.astype(f32)
    bu_r = bu_r.at[:, 0].add(ar * x0r - ai * x0i)
    bu_i = bu_i.at[:, 0].add(ar * x0i + ai * x0r)
    a_r = jnp.broadcast_to(ar, bu_r.shape)
    a_i = jnp.broadcast_to(ai, bu_i.shape)

    def combine(e1, e2):
        a1r, a1i, b1r, b1i = e1
        a2r, a2i, b2r, b2i = e2
        return (a2r * a1r - a2i * a1i, a2r * a1i + a2i * a1r,
                a2r * b1r - a2i * b1i + b2r, a2r * b1i + a2i * b1r + b2i)

    _, _, hr, hi = lax.associative_scan(combine, (a_r, a_i, bu_r, bu_i), axis=1)
    y = jnp.einsum('ntgp,gip->ntgi', hr, c_re.astype(f32)) - jnp.einsum('ntgp,gip->ntgi', hi, c_im.astype(f32))
    y = y.reshape(n, t, SSM_WIDTH) + d.astype(f32) * uf.reshape(n, t, SSM_WIDTH)
    z = jax.nn.gelu(y)
    out = z * jax.nn.sigmoid(z @ w_glu.astype(f32))
    return out.astype(u.dtype), hr[:, -1], hi[:, -1]


def ada_mods(c, w_ada, b_ada):
    m = jax.nn.silu(c) @ w_ada + b_ada
    return [mm[:, None, :] for mm in jnp.split(m, 6, axis=-1)]


def mixer_inputs(x, c, lw):
    mods = ada_mods(c, lw['w_ada'], lw['b_ada'])
    h = rms_norm(x, lw['g_norm_mix']) * (1 + mods[1]) + mods[0]
    n, t, _ = h.shape
    proj = jnp.einsum('ntd,de->nte', h, lw['w_in'])
    offs = [int(o) for o in np.cumsum(IN_SPLITS)[:-1]]
    q, kc, vc, ks, vs, kw, vw, g_nsa, u, g_a, g_s = jnp.split(proj, offs, axis=-1)
    q = (q * HEAD_DIM ** -0.5).reshape(n, t, N_KV_HEADS, Q_PER_KV, HEAD_DIM).transpose(0, 2, 3, 1, 4)
    kvs = [a.reshape(n, t, N_KV_HEADS, HEAD_DIM) for a in (kc, vc, ks, vs, kw, vw)]
    g_nsa = jax.nn.sigmoid(g_nsa.reshape(n, t, 3, N_Q_HEADS)).transpose(2, 0, 3, 1)
    g_nsa = g_nsa.reshape(3, n, N_KV_HEADS, Q_PER_KV, t)[..., None]
    return mods, q, kvs, g_nsa, u, g_a, g_s


def finish_layer(x, mods, o_cmp, o_slc, o_win, g_nsa, o_ssm, g_a, g_s, lw):
    n, g, r, t, hd = o_cmp.shape
    o = g_nsa[0] * o_cmp + g_nsa[1] * o_slc + g_nsa[2] * o_win
    o_attn = o.transpose(0, 3, 1, 2, 4).reshape(n, t, ATTN_WIDTH)
    mixed = jax.nn.sigmoid(g_a) * (o_attn @ lw['w_br_attn']) + jax.nn.sigmoid(g_s) * (o_ssm @ lw['w_br_ssm'])
    x = x + mods[2] * (mixed @ lw['w_out'])
    h = rms_norm(x, lw['g_norm_mlp']) * (1 + mods[4]) + mods[3]
    return x + mods[5] * (jnp.square(jax.nn.relu(h @ lw['w_ff1'])) @ lw['w_ff2'])


def ssm_args(lw):
    return (lw['ssm_lambda_re'], lw['ssm_lambda_im'], lw['ssm_log_dt'], lw['ssm_b_re'], lw['ssm_b_im'],
            lw['ssm_c_re'], lw['ssm_c_im'], lw['ssm_d'], lw['w_glu'])


def prompt_layer(x, c, lw, table):
    mods, q, (k_cmp, v_cmp, k_slc, v_slc, k_win, v_win), g_nsa, u, g_a, g_s = mixer_inputs(x, c, lw)
    n, t, _ = x.shape
    q_pos = jnp.arange(t)
    kc = compress(k_cmp, lw['cmp_pe'][0], lw['cmp_w1'][0], lw['cmp_b1'][0], lw['cmp_w2'][0])
    vc = compress(v_cmp, lw['cmp_pe'][1], lw['cmp_w1'][1], lw['cmp_b1'][1], lw['cmp_w2'][1])
    dist = q_pos[:, None] - cmp_ends(kc.shape[1])[None, :]
    o_cmp, p_cmp = attend_shared(q, kc, vc, dist, dist >= 0, table)
    idx, ok = select_blocks(jnp.sum(p_cmp, axis=2), q_pos, -(-t // SLC_BLOCK))
    o_slc = slc_prompt(q, k_slc, v_slc, idx, ok, table)
    o_win = win_prompt(q, k_win, v_win, table)
    zeros = jnp.zeros((n, N_SSM_GROUPS, SSM_STATE), jnp.float32)
    o_ssm, s_re, s_im = s5_ssm(u, zeros, zeros, *ssm_args(lw))
    y = finish_layer(x, mods, o_cmp, o_slc, o_win, g_nsa, o_ssm, g_a, g_s, lw)
    keep = min(WINDOW, t)
    return y, (k_cmp, v_cmp, k_slc, v_slc, k_win[:, t - keep:], v_win[:, t - keep:], s_re, s_im)


def sample_layer(x, c, lw, layer, cache_k_cmp, cache_v_cmp, cache_k_slc, cache_v_slc,
                 cache_k_win, cache_v_win, state_ssm_re, state_ssm_im, page_table, table):
    mods, q, (k_cmp, v_cmp, k_slc, v_slc, k_win, v_win), g_nsa, u, g_a, g_s = mixer_inputs(x, c, lw)
    n, t, _ = x.shape
    past_len = page_table.shape[1] * PAGE_SIZE
    q_pos = past_len + jnp.arange(t)
    k_past = cache_k_cmp[layer, page_table].reshape(n, past_len, N_KV_HEADS, HEAD_DIM)
    v_past = cache_v_cmp[layer, page_table].reshape(n, past_len, N_KV_HEADS, HEAD_DIM)
    k_all = jnp.concatenate([k_past, k_cmp.astype(k_past.dtype)], axis=1)
    v_all = jnp.concatenate([v_past, v_cmp.astype(v_past.dtype)], axis=1)
    kc = compress(k_all, lw['cmp_pe'][0], lw['cmp_w1'][0], lw['cmp_b1'][0], lw['cmp_w2'][0])
    vc = compress(v_all, lw['cmp_pe'][1], lw['cmp_w1'][1], lw['cmp_b1'][1], lw['cmp_w2'][1])
    dist = q_pos[:, None] - cmp_ends(kc.shape[1])[None, :]
    o_cmp, p_cmp = attend_shared(q, kc, vc, dist, dist >= 0, table)
    idx, ok = select_blocks(jnp.sum(p_cmp, axis=2), q_pos, -(-(past_len + t) // SLC_BLOCK))
    o_slc = slc_sample(q, k_slc, v_slc, idx, ok, q_pos, cache_k_slc, cache_v_slc, layer, page_table, table)
    o_win, nbk, nbv = win_sample(q, k_win, v_win, cache_k_win[layer], cache_v_win[layer], q_pos, past_len, table)
    o_ssm, s_re, s_im = s5_ssm(u, state_ssm_re[layer], state_ssm_im[layer], *ssm_args(lw))
    y = finish_layer(x, mods, o_cmp, o_slc, o_win, g_nsa, o_ssm, g_a, g_s, lw)
    return y, (k_cmp, v_cmp, k_slc, v_slc, nbk, nbv, s_re, s_im)


def setup_inputs(seed: int = 0) -> dict:
    key = jax.random.key(seed)
    ks = iter(jax.random.split(key, 64))

    def normal(shape, scale):
        return scale * jax.random.normal(next(ks), shape, jnp.float32)

    n_pages = PAST_LEN // PAGE_SIZE
    n_phys = (5 * DEC_BATCH * n_pages) // 4
    w_buf = min(WINDOW, PAST_LEN)
    page_shape = (DEPTH, n_phys, PAGE_SIZE, N_KV_HEADS, HEAD_DIM)
    win_shape = (DEPTH, DEC_BATCH, w_buf, N_KV_HEADS, HEAD_DIM)
    ssm_shape = (DEPTH, DEC_BATCH, N_SSM_GROUPS, SSM_STATE)
    page_table = jax.random.permutation(next(ks), n_phys)[:DEC_BATCH * n_pages]
    page_table = page_table.reshape(DEC_BATCH, n_pages).astype(jnp.int32)
    lam_im = jnp.pi * jnp.arange(SSM_STATE, dtype=jnp.float32)
    log_dt = jax.random.uniform(next(ks), (DEPTH, N_SSM_GROUPS), jnp.float32,
                                math.log(DT_MIN), math.log(DT_MAX))
    return {
        'x_prompt': normal((BATCH, SEQ, D_MODEL), 1.0),
        'x_sample': normal((DEC_BATCH, DEC_SEQ, D_MODEL), 1.0),
        'cache_k_cmp': normal(page_shape, 1.0),
        'cache_v_cmp': normal(page_shape, 1.0),
        'cache_k_slc': normal(page_shape, 1.0),
        'cache_v_slc': normal(page_shape, 1.0),
        'cache_k_win': normal(win_shape, 1.0),
        'cache_v_win': normal(win_shape, 1.0),
        'state_ssm_re': normal(ssm_shape, 0.3),
        'state_ssm_im': normal(ssm_shape, 0.3),
        'page_table': page_table,
        'c_prompt': normal((BATCH, D_MODEL), 1.0),
        'c_sample': normal((DEC_BATCH, D_MODEL), 1.0),
        'rel_bias': normal((N_BUCKETS, N_Q_HEADS), 0.5),
        'w_ada': normal((DEPTH, D_MODEL, 6 * D_MODEL), 0.5 * D_MODEL ** -0.5),
        'b_ada': normal((DEPTH, 6 * D_MODEL), 0.02),
        'g_norm_mix': 1.0 + normal((DEPTH, D_MODEL), 0.02),
        'g_norm_mlp': 1.0 + normal((DEPTH, D_MODEL), 0.02),
        'w_in': normal((DEPTH, D_MODEL, IN_WIDTH), D_MODEL ** -0.5),
        'cmp_pe': normal((DEPTH, 2, CMP_BLOCK, HEAD_DIM), 0.02),
        'cmp_w1': normal((DEPTH, 2, CMP_BLOCK, HEAD_DIM, HEAD_DIM), (CMP_BLOCK * HEAD_DIM) ** -0.5),
        'cmp_b1': normal((DEPTH, 2, HEAD_DIM), 0.02),
        'cmp_w2': normal((DEPTH, 2, HEAD_DIM, HEAD_DIM), HEAD_DIM ** -0.5),
        'ssm_lambda_re': -0.5 + normal((DEPTH, N_SSM_GROUPS, SSM_STATE), 0.01),
        'ssm_lambda_im': lam_im + normal((DEPTH, N_SSM_GROUPS, SSM_STATE), 0.01),
        'ssm_log_dt': log_dt,
        'ssm_b_re': normal((DEPTH, N_SSM_GROUPS, SSM_STATE, SSM_GROUP), (2 * SSM_GROUP) ** -0.5),
        'ssm_b_im': normal((DEPTH, N_SSM_GROUPS, SSM_STATE, SSM_GROUP), (2 * SSM_GROUP) ** -0.5),
        'ssm_c_re': normal((DEPTH, N_SSM_GROUPS, SSM_GROUP, SSM_STATE), (2 * SSM_STATE) ** -0.5),
        'ssm_c_im': normal((DEPTH, N_SSM_GROUPS, SSM_GROUP, SSM_STATE), (2 * SSM_STATE) ** -0.5),
        'ssm_d': normal((DEPTH, SSM_WIDTH), 0.5),
        'w_glu': normal((DEPTH, SSM_WIDTH, SSM_WIDTH), SSM_WIDTH ** -0.5),
        'w_br_attn': normal((DEPTH, ATTN_WIDTH, D_MODEL), ATTN_WIDTH ** -0.5),
        'w_br_ssm': normal((DEPTH, SSM_WIDTH, D_MODEL), SSM_WIDTH ** -0.5),
        'w_out': normal((DEPTH, D_MODEL, D_MODEL), D_MODEL ** -0.5),
        'w_ff1': normal((DEPTH, D_MODEL, D_FF), D_MODEL ** -0.5),
        'w_ff2': normal((DEPTH, D_FF, D_MODEL), D_FF ** -0.5),
        'g_final': 1.0 + normal((D_MODEL,), 0.02),
    }


def reference(x_prompt, x_sample, cache_k_cmp, cache_v_cmp, cache_k_slc, cache_v_slc, cache_k_win,
              cache_v_win, state_ssm_re, state_ssm_im, page_table, c_prompt, c_sample, rel_bias,
              w_ada, b_ada, g_norm_mix, g_norm_mlp, w_in, cmp_pe, cmp_w1, cmp_b1, cmp_w2,
              ssm_lambda_re, ssm_lambda_im, ssm_log_dt, ssm_b_re, ssm_b_im, ssm_c_re, ssm_c_im,
              ssm_d, w_glu, w_br_attn, w_br_ssm, w_out, w_ff1, w_ff2, g_final):
    hp, hs = x_prompt, x_sample
    p_states, s_states = [], []
    for l in range(DEPTH):
        lw = dict(w_ada=w_ada[l], b_ada=b_ada[l], g_norm_mix=g_norm_mix[l], g_norm_mlp=g_norm_mlp[l],
                  w_in=w_in[l], cmp_pe=cmp_pe[l], cmp_w1=cmp_w1[l], cmp_b1=cmp_b1[l], cmp_w2=cmp_w2[l],
                  ssm_lambda_re=ssm_lambda_re[l], ssm_lambda_im=ssm_lambda_im[l], ssm_log_dt=ssm_log_dt[l],
                  ssm_b_re=ssm_b_re[l], ssm_b_im=ssm_b_im[l], ssm_c_re=ssm_c_re[l], ssm_c_im=ssm_c_im[l],
                  ssm_d=ssm_d[l], w_glu=w_glu[l], w_br_attn=w_br_attn[l], w_br_ssm=w_br_ssm[l],
                  w_out=w_out[l], w_ff1=w_ff1[l], w_ff2=w_ff2[l])
        hp, st = prompt_layer(hp, c_prompt, lw, rel_bias)
        p_states.append(st)
        hs, st = sample_layer(hs, c_sample, lw, l, cache_k_cmp, cache_v_cmp, cache_k_slc, cache_v_slc,
                              cache_k_win, cache_v_win, state_ssm_re, state_ssm_im, page_table, rel_bias)
        s_states.append(st)
    p_k_cmp, p_v_cmp, p_k_slc, p_v_slc, p_k_win, p_v_win, p_ssm_re, p_ssm_im = [jnp.stack(a) for a in zip(*p_states)]
    s_k_cmp, s_v_cmp, s_k_slc, s_v_slc, s_k_win, s_v_win, s_ssm_re, s_ssm_im = [jnp.stack(a) for a in zip(*s_states)]
    y_prompt = rms_norm(hp, g_final)
    y_sample = rms_norm(hs, g_final)
    return (y_prompt, y_sample,
            p_k_cmp, p_v_cmp, p_k_slc, p_v_slc, p_k_win, p_v_win, p_ssm_re, p_ssm_im,
            s_k_cmp, s_v_cmp, s_k_slc, s_v_slc, s_k_win, s_v_win, s_ssm_re, s_ssm_im)
```

```python
import functools
import math

import numpy as np
import jax
import jax.numpy as jnp
from jax import lax
from jax.experimental import pallas as pl
from jax.experimental.pallas import tpu as pltpu

F32 = jnp.float32
BF16 = jnp.bfloat16

D_MODEL = 1024
PAGE_SIZE = 128
N_Q_HEADS = 8
N_KV_HEADS = 2
Q_PER_KV = N_Q_HEADS // N_KV_HEADS
HEAD_DIM = 64
ATTN_WIDTH = N_Q_HEADS * HEAD_DIM
KV_WIDTH = N_KV_HEADS * HEAD_DIM
CMP_BLOCK = 32
CMP_STRIDE = 16
CMP_RATIO = CMP_BLOCK // CMP_STRIDE
SLC_BLOCK = 64
SLC_SHIFT = 6
N_SELECT = 16
WINDOW = 512
SSM_WIDTH = D_MODEL // 2
SSM_GROUP = 16
N_SSM_GROUPS = SSM_WIDTH // SSM_GROUP
SSM_STATE = 64
SSM_CH = N_SSM_GROUPS * SSM_STATE
N_BUCKETS = 32
MAX_DISTANCE = 128
D_FF = 4 * D_MODEL
IN_SPLITS = (ATTN_WIDTH,) + (KV_WIDTH,) * 6 + (3 * N_Q_HEADS, SSM_WIDTH, D_MODEL, D_MODEL)
RMS_EPS = 1e-6
NEG_INF = -1e30
TINY = 1e-30

LANES = 128
SUBLANES = 8
VMEM_LIMIT = 56 * 1024 * 1024
TQ = 256
CHUNK_W = CMP_STRIDE * KV_WIDTH


def _bucket_thresholds():
    n = np.arange(0, MAX_DISTANCE + 1)
    exact = N_BUCKETS // 2
    nf = np.maximum(n, exact).astype(np.float64)
    large = exact + (np.log(nf / exact) / math.log(MAX_DISTANCE / exact) * (N_BUCKETS - exact)).astype(np.int64)
    bucket = np.where(n < exact, n, np.minimum(large, N_BUCKETS - 1))
    return [int(np.argmax(bucket >= b)) for b in range(1, N_BUCKETS)]


BUCKET_THR = _bucket_thresholds()
FAR_DIST = BUCKET_THR[-1]
assert TQ + 1 >= FAR_DIST


def _cparams(sem):
    return pltpu.CompilerParams(dimension_semantics=sem, vmem_limit_bytes=VMEM_LIMIT)


def _dot(a, b):
    return jnp.dot(a, b, preferred_element_type=F32)


def _dot_t(a, b):
    return lax.dot_general(a, b, (((1,), (1,)), ((), ())), preferred_element_type=F32)


def _sigmoid(x):
    return jax.nn.sigmoid(x)


def _rms(x, g):
    ms = jnp.mean(x * x, axis=-1, keepdims=True)
    return x * lax.rsqrt(ms + RMS_EPS) * g


def _bias_kernel(tab_ref, dist_ref, out_ref):
    h = pl.program_id(0)
    d = dist_ref[0]
    b = jnp.full(d.shape, tab_ref[0, h], F32)
    for k, thr in enumerate(BUCKET_THR):
        b = jnp.where(d >= thr, tab_ref[k + 1, h], b)
    out_ref[0, 0] = jnp.where(d < 0, NEG_INF, b)


def _bias_from_dist(rel_bias, dist):
    a, r, c = dist.shape
    return pl.pallas_call(
        _bias_kernel,
        out_shape=jax.ShapeDtypeStruct((N_Q_HEADS, a, r, c), F32),
        grid=(N_Q_HEADS, a),
        in_specs=[pl.BlockSpec(memory_space=pltpu.SMEM),
                  pl.BlockSpec((1, r, c), lambda h, i: (i, 0, 0))],
        out_specs=pl.BlockSpec((1, 1, r, c), lambda h, i: (h, i, 0, 0)),
        compiler_params=_cparams(("arbitrary", "arbitrary")),
        name="t5_bias",
    )(rel_bias, dist)


def _mods_kernel(c_ref, w_ref, b_ref, o_ref):
    c = c_ref[...]
    s = c * _sigmoid(c)
    o_ref[...] = _dot(s.astype(BF16), w_ref[...].astype(BF16)) + b_ref[...]


def _ada_mods(c_all, w_ada, b_ada):
    n = c_all.shape[0]
    return pl.pallas_call(
        _mods_kernel,
        out_shape=jax.ShapeDtypeStruct((n, 6 * D_MODEL), F32),
        grid=(6,),
        in_specs=[pl.BlockSpec((n, D_MODEL), lambda j: (0, 0)),
                  pl.BlockSpec((D_MODEL, D_MODEL), lambda j: (0, j)),
                  pl.BlockSpec((1, D_MODEL), lambda j: (0, j))],
        out_specs=pl.BlockSpec((n, D_MODEL), lambda j: (0, j)),
        compiler_params=_cparams(("arbitrary",)),
        name="ada_mods",
    )(c_all, w_ada, b_ada.reshape(1, -1))


Q_COLS = N_Q_HEADS * LANES
KV_COLS = 6 * KV_WIDTH
OFF_KV = Q_COLS
OFF_U = OFF_KV + KV_COLS
OFF_GA = OFF_U + SSM_WIDTH
OFF_GS = OFF_GA + D_MODEL
OFF_GN = OFF_GS + D_MODEL
W_COLS = OFF_GN + LANES


def _permute_w_in(w_in):
    offs = np.cumsum((0,) + IN_SPLITS)
    wq = w_in[:, :ATTN_WIDTH].reshape(D_MODEL, N_Q_HEADS, HEAD_DIM)
    low = (np.arange(N_Q_HEADS) < Q_PER_KV)[None, :, None]
    wq = jnp.concatenate([jnp.where(low, wq, 0.0), jnp.where(low, 0.0, wq)], axis=-1)
    wq = wq.reshape(D_MODEL, Q_COLS)
    wkv = w_in[:, offs[1]:offs[7]]
    wgn = jnp.pad(w_in[:, offs[7]:offs[8]], ((0, 0), (0, LANES - 3 * N_Q_HEADS)))
    wrest = w_in[:, offs[8]:]
    return jnp.concatenate([wq, wkv, wrest, wgn], axis=1).astype(BF16)


def _inproj_kernel(x_ref, sh_ref, sc_ref, g_ref, w_ref, q_ref, kv_ref, u_ref, ga_ref, gs_ref, gn_ref):
    h = _rms(x_ref[...], g_ref[...]) * (1.0 + sc_ref[...]) + sh_ref[...]
    hb = h.astype(BF16)
    pq = _dot(hb, w_ref[:, 0:OFF_KV]) * (HEAD_DIM ** -0.5)
    for i in range(N_Q_HEADS):
        q_ref[i] = pq[:, i * LANES:(i + 1) * LANES]
    pkv = _dot(hb, w_ref[:, OFF_KV:OFF_U])
    for i in range(6):
        kv_ref[i] = pkv[:, i * KV_WIDTH:(i + 1) * KV_WIDTH]
    u_ref[...] = _dot(hb, w_ref[:, OFF_U:OFF_GA])
    ga_ref[...] = _sigmoid(_dot(hb, w_ref[:, OFF_GA:OFF_GS]))
    gs_ref[...] = _sigmoid(_dot(hb, w_ref[:, OFF_GS:OFF_GN]))
    gn_ref[...] = _sigmoid(_dot(hb, w_ref[:, OFF_GN:W_COLS]))


def _inproj(x2, mods, g_mix, w_perm, *, nb, t, time_major):
    rows = x2.shape[0]
    if time_major:
        tm = nb
        grid = (1, t)
        row_blk = lambda i, j: j
        mod_spec = lambda k: pl.BlockSpec((nb, D_MODEL), lambda i, j: (0, k))
        u_shape = (rows, SSM_WIDTH)
        u_spec = pl.BlockSpec((tm, SSM_WIDTH), lambda i, j: (j, 0))
    else:
        tm = min(TQ, t)
        nt = t // tm
        grid = (nb, nt)
        row_blk = lambda i, j: i * nt + j
        mod_spec = lambda k: pl.BlockSpec((None, 1, D_MODEL), lambda i, j: (i, 0, k))
        u_shape = (t, nb * SSM_WIDTH)
        u_spec = pl.BlockSpec((tm, SSM_WIDTH), lambda i, j: (j, i))
    rspec = lambda w: pl.BlockSpec((tm, w), lambda i, j: (row_blk(i, j), 0))
    hspec = lambda k: pl.BlockSpec((k, tm, LANES), lambda i, j: (0, row_blk(i, j), 0))
    return pl.pallas_call(
        _inproj_kernel,
        out_shape=(jax.ShapeDtypeStruct((N_Q_HEADS, rows, LANES), F32),
                   jax.ShapeDtypeStruct((6, rows, KV_WIDTH), F32),
                   jax.ShapeDtypeStruct(u_shape, F32),
                   jax.ShapeDtypeStruct((rows, D_MODEL), F32),
                   jax.ShapeDtypeStruct((rows, D_MODEL), F32),
                   jax.ShapeDtypeStruct((rows, LANES), F32)),
        grid=grid,
        in_specs=[rspec(D_MODEL), mod_spec(0), mod_spec(1),
                  pl.BlockSpec((1, D_MODEL), lambda i, j: (0, 0)),
                  pl.BlockSpec((D_MODEL, W_COLS), lambda i, j: (0, 0), pipeline_mode=pl.Buffered(1))],
        out_specs=(hspec(N_Q_HEADS), hspec(6), u_spec, rspec(D_MODEL), rspec(D_MODEL), rspec(LANES)),
        compiler_params=_cparams(("arbitrary", "arbitrary")),
        name="in_proj",
    )(x2, mods, mods, g_mix.reshape(1, -1), w_perm)


def _blockdiag2(w):
    z = jnp.zeros_like(w)
    return jnp.concatenate([jnp.concatenate([w, z], axis=-1), jnp.concatenate([z, w], axis=-1)], axis=-2)


def _compress_weights(cmp_pe, cmp_w1, cmp_b1, cmp_w2):
    w1 = _blockdiag2(cmp_w1.reshape(2, CMP_RATIO, CMP_STRIDE, HEAD_DIM, HEAD_DIM))
    w1 = w1.transpose(0, 2, 3, 1, 4).reshape(2, CHUNK_W, CMP_RATIO * KV_WIDTH).astype(BF16)
    pe = jnp.tile(cmp_pe.reshape(2, CMP_RATIO, CMP_STRIDE, 1, HEAD_DIM), (1, 1, 1, N_KV_HEADS, 1))
    pe = jnp.pad(pe.reshape(2, CMP_RATIO, CHUNK_W), ((0, 0), (0, SUBLANES - CMP_RATIO), (0, 0)))
    b1 = jnp.tile(cmp_b1, (1, N_KV_HEADS)).reshape(2, 1, KV_WIDTH)
    w2 = _blockdiag2(cmp_w2).astype(BF16)
    return w1, pe, b1, w2


def _compress_core(xb, new_p1, w1, pe, b1, w2):
    c = xb.shape[0]
    part = _dot(xb, w1)
    pc = _dot(pe.astype(BF16), w1)
    const = b1 + pc[0:1, :KV_WIDTH] + pc[1:2, KV_WIDTH:]
    p1 = pltpu.roll(part[:, KV_WIDTH:], c - 1, 0)
    row = lax.broadcasted_iota(jnp.int32, (c, KV_WIDTH), 0)
    if new_p1 is not None:
        p1 = jnp.where(row == c - 1, new_p1, p1)
    hid = const + part[:, :KV_WIDTH] + p1
    out = _dot((hid * _sigmoid(hid)).astype(BF16), w2)
    if new_p1 is None:
        out = jnp.where(row == c - 1, 0.0, out)
    return out


def _compress_kernel(x_ref, w1_ref, pe_ref, b1_ref, w2_ref, o_ref):
    o_ref[...] = _compress_core(x_ref[...].astype(BF16), None, w1_ref[...], pe_ref[...], b1_ref[...], w2_ref[...])


def _compress_prompt(xkv, w1, pe, b1, w2):
    _, nb, c, _ = xkv.shape
    wspec = lambda a, b: pl.BlockSpec((None, a, b), lambda i, n: (i, 0, 0))
    return pl.pallas_call(
        _compress_kernel,
        out_shape=jax.ShapeDtypeStruct((2, nb, c, KV_WIDTH), F32),
        grid=(2, nb),
        in_specs=[pl.BlockSpec((None, None, c, CHUNK_W), lambda i, n: (i, n, 0, 0)),
                  wspec(CHUNK_W, CMP_RATIO * KV_WIDTH), wspec(SUBLANES, CHUNK_W), wspec(1, KV_WIDTH),
                  wspec(KV_WIDTH, KV_WIDTH)],
        out_specs=pl.BlockSpec((None, None, c, KV_WIDTH), lambda i, n: (i, n, 0, 0)),
        compiler_params=_cparams(("arbitrary", "arbitrary")),
        name="compress_prompt",
    )(xkv, w1, pe, b1, w2)


def _split3_dot(x, w):
    hi = x.astype(BF16)
    r1 = x - hi.astype(F32)
    mid = r1.astype(BF16)
    lo = (r1 - mid.astype(F32)).astype(BF16)
    return _dot(hi, w) + _dot(mid, w) + _dot(lo, w)


def _select_mask(score, q_pos, n_slc):
    r, l = score.shape
    blk = lax.broadcasted_iota(jnp.int32, (r, l), 1)
    cur = lax.shift_right_arithmetic(q_pos, SLC_SHIFT)
    forced = (blk == 0) | (blk == cur) | (blk == cur - 1)
    avail = blk * SLC_BLOCK <= q_pos
    sc = jnp.where(forced, jnp.inf, jnp.where(avail, score, -jnp.inf))
    sc = jnp.where(blk < n_slc, sc, -jnp.inf)
    cnt = jnp.zeros((r, l), F32)
    for j in range(n_slc):
        col = sc[:, j:j + 1]
        ahead = (col > sc) | ((col == sc) & (blk > j))
        cnt = cnt + jnp.where(ahead, 1.0, 0.0)
    sel = (cnt < float(min(N_SELECT, n_slc))) & (sc > -jnp.inf)
    return jnp.where(blk < n_slc, jnp.where(sel, 0.0, NEG_INF), 0.0)


def _overlap_matrix(n_cmp, n_slc, rows, cols):
    c = np.arange(n_cmp)[:, None] + np.arange(CMP_RATIO)[None, :]
    cb = c // (SLC_BLOCK // CMP_STRIDE)
    ov = (cb[:, :, None] == np.arange(n_slc)[None, None, :]).sum(axis=1)
    out = np.zeros((rows, cols), np.float32)
    out[:n_cmp, :n_slc] = ov
    return jnp.asarray(out, BF16)


def _attn_prompt_kernel(tab_ref, q_ref, kc_ref, vc_ref, bcmp_ref, ks_ref, vs_ref, kw_ref, vw_ref,
                        bt_ref, gn_ref, ovl_ref, o_ref,
                        kaug, vsb, kwb, vwb, m_sc, l_sc, acc_sc, *, t, n_slc):
    g = pl.program_id(1)
    qi = pl.program_id(2)
    rq = Q_PER_KV * TQ

    @pl.when((g == 0) & (qi == 0))
    def _():
        kaug[:, 0:LANES] = ks_ref[...].astype(BF16)
        blk = lax.broadcasted_iota(jnp.int32, (t, LANES), 1)
        pos = lax.broadcasted_iota(jnp.int32, (t, LANES), 0)
        kaug[:, LANES:2 * LANES] = jnp.where(blk == lax.shift_right_arithmetic(pos, SLC_SHIFT), 1.0, 0.0).astype(BF16)
        vsb[...] = vs_ref[...].astype(BF16)
        kwb[...] = kw_ref[...].astype(BF16)
        vwb[...] = vw_ref[...].astype(BF16)

    qb = q_ref[...].reshape(rq, LANES).astype(BF16)

    bc = bcmp_ref[...].reshape(rq, LANES)
    valid = bc > 0.5 * NEG_INF
    s = jnp.where(valid, _dot_t(qb, kc_ref[...].astype(BF16)) + bc, NEG_INF)
    p = jnp.where(valid, jnp.exp(s - jnp.max(s, axis=-1, keepdims=True)), 0.0)
    p = p / jnp.maximum(jnp.sum(p, axis=-1, keepdims=True), TINY)
    o_cmp = _dot(p.astype(BF16), vc_ref[...].astype(BF16))
    imp = p[0:TQ]
    for r in range(1, Q_PER_KV):
        imp = imp + p[r * TQ:(r + 1) * TQ]
    score = _split3_dot(imp, ovl_ref[...])
    q_pos = qi * TQ + lax.broadcasted_iota(jnp.int32, (TQ, 1), 0)
    selq = _select_mask(score, q_pos, n_slc)
    selq4 = jnp.concatenate([selq] * Q_PER_KV, axis=0).astype(BF16)
    qaug = jnp.concatenate([qb, selq4], axis=-1)

    def reset():
        m_sc[...] = jnp.full((rq, 1), NEG_INF, F32)
        l_sc[...] = jnp.zeros((rq, 1), F32)
        acc_sc[...] = jnp.zeros((rq, LANES), F32)

    def update(qq, k_tile, v_tile, bias):
        sc = _dot_t(qq, k_tile) + bias
        m_old = m_sc[...]
        m_new = jnp.maximum(m_old, jnp.max(sc, axis=-1, keepdims=True))
        alpha = jnp.exp(m_old - m_new)
        pp = jnp.exp(sc - m_new)
        l_sc[...] = alpha * l_sc[...] + jnp.sum(pp, axis=-1, keepdims=True)
        acc_sc[...] = alpha * acc_sc[...] + _dot(pp.astype(BF16), v_tile)
        m_sc[...] = m_new

    def finish():
        return acc_sc[...] / jnp.maximum(l_sc[...], TINY)

    def rows_at(off):
        return pl.ds(pl.multiple_of((qi - off) * TQ, TQ), TQ)

    def near_bias(off):
        return bt_ref[:, off].reshape(rq, TQ)

    reset()
    update(qaug, kaug[rows_at(0), :], vsb[rows_at(0), :], near_bias(0))

    @pl.when(qi >= 1)
    def _():
        update(qaug, kaug[rows_at(1), :], vsb[rows_at(1), :], near_bias(1))

    far = jnp.concatenate(
        [jnp.full((TQ, 1), tab_ref[N_BUCKETS - 1, g * Q_PER_KV + r], F32) for r in range(Q_PER_KV)], axis=0)

    def far_body(j, carry):
        rows = pl.ds(pl.multiple_of(j * TQ, TQ), TQ)
        update(qaug, kaug[rows, :], vsb[rows, :], far)
        return carry

    lax.fori_loop(0, jnp.maximum(qi - 1, 0), far_body, 0)
    o_slc = finish()

    reset()
    update(qb, kwb[rows_at(0), :], vwb[rows_at(0), :], near_bias(0))
    for off in (1, 2):
        @pl.when(qi >= off)
        def _():
            update(qb, kwb[rows_at(off), :], vwb[rows_at(off), :], near_bias(off))
    o_win = finish()

    gn = gn_ref[...]
    lane = lax.broadcasted_iota(jnp.int32, (TQ, LANES), 1)
    heads = []
    for r in range(Q_PER_KV):
        rs = slice(r * TQ, (r + 1) * TQ)
        acc = jnp.zeros((TQ, LANES), F32)
        for b, ob in enumerate((o_cmp, o_slc, o_win)):
            acc = acc + ob[rs] * _gate_column(gn, lane, b, g * Q_PER_KV + r)
        heads.append(acc)
    for pair in range(Q_PER_KV // 2):
        a, b = heads[2 * pair], heads[2 * pair + 1]
        lo = jnp.where(g == 0, a, pltpu.roll(a, HEAD_DIM, 1))
        hi = jnp.where(g == 0, pltpu.roll(b, HEAD_DIM, 1), b)
        o_ref[:, pair * LANES:(pair + 1) * LANES] = jnp.where(lane < HEAD_DIM, lo, hi)


def _gate_column(gn, lane, branch, head):
    return jnp.sum(jnp.where(lane == branch * N_Q_HEADS + head, gn, 0.0), axis=-1, keepdims=True)


def _attn_prompt(rel_bias, q, kcvc, bcmp, kv, bt, gn, ovl, *, nb, t):
    nq = t // TQ
    n_slc = t // SLC_BLOCK
    c = kcvc.shape[2]
    rq = Q_PER_KV * TQ
    kvspec = lambda i: pl.BlockSpec((None, None, t, KV_WIDTH), lambda n, g, j: (i, n, 0, 0))
    kern = functools.partial(_attn_prompt_kernel, t=t, n_slc=n_slc)
    return pl.pallas_call(
        kern,
        out_shape=jax.ShapeDtypeStruct((nb * t, ATTN_WIDTH), F32),
        grid=(nb, N_KV_HEADS, nq),
        in_specs=[pl.BlockSpec(memory_space=pltpu.SMEM),
                  pl.BlockSpec((Q_PER_KV, TQ, LANES), lambda n, g, j: (g, n * nq + j, 0)),
                  pl.BlockSpec((None, None, c, KV_WIDTH), lambda n, g, j: (0, n, 0, 0)),
                  pl.BlockSpec((None, None, c, KV_WIDTH), lambda n, g, j: (1, n, 0, 0)),
                  pl.BlockSpec((Q_PER_KV, TQ, LANES), lambda n, g, j: (g, j, 0)),
                  kvspec(2), kvspec(3), kvspec(4), kvspec(5),
                  pl.BlockSpec((Q_PER_KV, 3, TQ, TQ), lambda n, g, j: (g, 0, 0, 0)),
                  pl.BlockSpec((TQ, LANES), lambda n, g, j: (n * nq + j, 0)),
                  pl.BlockSpec((c, LANES), lambda n, g, j: (0, 0))],
        out_specs=pl.BlockSpec((TQ, Q_PER_KV * HEAD_DIM), lambda n, g, j: (n * nq + j, g)),
        scratch_shapes=[pltpu.VMEM((t, 2 * LANES), BF16), pltpu.VMEM((t, LANES), BF16),
                        pltpu.VMEM((t, LANES), BF16), pltpu.VMEM((t, LANES), BF16),
                        pltpu.VMEM((rq, 1), F32), pltpu.VMEM((rq, 1), F32), pltpu.VMEM((rq, LANES), F32)],
        compiler_params=_cparams(("arbitrary", "arbitrary", "arbitrary")),
        name="attn_prompt",
    )(rel_bias, q, kcvc, kcvc, bcmp, kv, kv, kv, kv, bt, gn, ovl)


def _ssm_kernel(u_ref, x0r_ref, x0i_ref, lr_ref, li_ref, dt_ref, br_ref, bi_ref, cr_ref, ci_ref,
                d_ref, wg_ref, o_ref, hr_ref, hi_ref,
                bbr, bbi, crb, cib, a_sc, h_sc, bur, bui, *, nb, tt, lc):
    step = pl.program_id(0)

    @pl.when(step == 0)
    def _():
        dt = jnp.exp(dt_ref[...])
        lr = jnp.minimum(lr_ref[...], -1e-4)
        li = li_ref[...]
        mag = jnp.exp(lr * dt)
        ar = mag * jnp.cos(li * dt)
        ai = mag * jnp.sin(li * dt)
        den = lr * lr + li * li
        fr = ((ar - 1.0) * lr + ai * li) / den
        fi = (ai * lr - (ar - 1.0) * li) / den
        a_sc[0:1, :] = ar
        a_sc[1:2, :] = ai
        bbr[...] = (fr * br_ref[...] - fi * bi_ref[...]).astype(BF16)
        bbi[...] = (fr * bi_ref[...] + fi * br_ref[...]).astype(BF16)
        crb[...] = cr_ref[...].astype(BF16)
        cib[...] = ci_ref[...].astype(BF16)
        h_sc[0] = x0r_ref[...]
        h_sc[1] = x0i_ref[...]

    u = u_ref[...]
    ub = u.astype(BF16)
    bur[...] = _dot(ub, bbr[...])
    bui[...] = _dot(ub, bbi[...])

    for c in range(SSM_CH // lc):
        cs = slice(c * lc, (c + 1) * lc)
        ar = jnp.broadcast_to(a_sc[0:1, cs], (nb, lc))
        ai = jnp.broadcast_to(a_sc[1:2, cs], (nb, lc))

        def body(s, carry):
            hr, hi = carry
            rows = pl.ds(pl.multiple_of(s * nb, nb), nb)
            nr = ar * hr - ai * hi + bur[rows, cs]
            ni = ar * hi + ai * hr + bui[rows, cs]
            bur[rows, cs] = nr
            bui[rows, cs] = ni
            return nr, ni

        hr, hi = lax.fori_loop(0, tt, body, (h_sc[0, :, cs], h_sc[1, :, cs]))
        h_sc[0, :, cs] = hr
        h_sc[1, :, cs] = hi

    y = _dot(bur[...].astype(BF16), crb[...]) - _dot(bui[...].astype(BF16), cib[...]) + d_ref[...] * u
    z = 0.5 * y * (1.0 + jnp.tanh(math.sqrt(2.0 / math.pi) * (y + 0.044715 * (y * y * y))))
    o_ref[...] = z * _sigmoid(_dot(z.astype(BF16), wg_ref[...]))

    @pl.when(step == pl.num_programs(0) - 1)
    def _():
        hr_ref[...] = h_sc[0]
        hi_ref[...] = h_sc[1]


def _ssm_weights(ssm_b_re, ssm_b_im, ssm_c_re, ssm_c_im):
    eye = jnp.eye(N_SSM_GROUPS, dtype=F32)

    def b_blk(b):
        return jnp.einsum('gpi,gh->gihp', b, eye).reshape(SSM_WIDTH, SSM_CH)

    def c_blk(c):
        return jnp.einsum('gip,gh->gphi', c, eye).reshape(SSM_CH, SSM_WIDTH)

    return b_blk(ssm_b_re), b_blk(ssm_b_im), c_blk(ssm_c_re), c_blk(ssm_c_im)


def _ssm(u_tm, x0r, x0i, lam_re, lam_im, log_dt, bblk, d, w_glu_b, *, nb, t):
    tt = max(1, min(t, 512 // nb))
    lc = min(SSM_CH, max(LANES, 4096 // nb))
    rows = tt * nb
    full = lambda a, b: pl.BlockSpec((a, b), lambda s: (0, 0), pipeline_mode=pl.Buffered(1))
    kern = functools.partial(_ssm_kernel, nb=nb, tt=tt, lc=lc)
    br, bi, cr, ci = bblk
    return pl.pallas_call(
        kern,
        out_shape=(jax.ShapeDtypeStruct((t * nb, SSM_WIDTH), F32),
                   jax.ShapeDtypeStruct((nb, SSM_CH), F32),
                   jax.ShapeDtypeStruct((nb, SSM_CH), F32)),
        grid=(t // tt,),
        in_specs=[pl.BlockSpec((rows, SSM_WIDTH), lambda s: (s, 0)),
                  full(nb, SSM_CH), full(nb, SSM_CH),
                  full(1, SSM_CH), full(1, SSM_CH), full(1, SSM_CH),
                  full(SSM_WIDTH, SSM_CH), full(SSM_WIDTH, SSM_CH),
                  full(SSM_CH, SSM_WIDTH), full(SSM_CH, SSM_WIDTH),
                  full(1, SSM_WIDTH), full(SSM_WIDTH, SSM_WIDTH)],
        out_specs=(pl.BlockSpec((rows, SSM_WIDTH), lambda s: (s, 0)),
                   pl.BlockSpec((nb, SSM_CH), lambda s: (0, 0)), pl.BlockSpec((nb, SSM_CH), lambda s: (0, 0))),
        scratch_shapes=[pltpu.VMEM((SSM_WIDTH, SSM_CH), BF16), pltpu.VMEM((SSM_WIDTH, SSM_CH), BF16),
                        pltpu.VMEM((SSM_CH, SSM_WIDTH), BF16), pltpu.VMEM((SSM_CH, SSM_WIDTH), BF16),
                        pltpu.VMEM((SUBLANES, SSM_CH), F32), pltpu.VMEM((2, nb, SSM_CH), F32),
                        pltpu.VMEM((rows, SSM_CH), F32), pltpu.VMEM((rows, SSM_CH), F32)],
        compiler_params=_cparams(("arbitrary",)),
        name="s5_ssm",
    )(u_tm, x0r, x0i, lam_re.reshape(1, -1), lam_im.reshape(1, -1),
      jnp.repeat(log_dt, SSM_STATE).reshape(1, -1), br, bi, cr, ci, d.reshape(1, -1), w_glu_b)


FF_CHUNK = 1024


def _finish_kernel(x_ref, oa_ref, os_ref, ga_ref, gs_ref, m2_ref, m3_ref, m4_ref, m5_ref,
                   gmlp_ref, gfin_ref, wba_ref, wbs_ref, wout_ref, w1_ref, w2_ref, y_ref):
    a = _dot(oa_ref[...].astype(BF16), wba_ref[...])
    s = _dot(os_ref[...].astype(BF16), wbs_ref[...])
    mixed = ga_ref[...] * a + gs_ref[...] * s
    x1 = x_ref[...] + m2_ref[...] * _dot(mixed.astype(BF16), wout_ref[...])
    hb = (_rms(x1, gmlp_ref[...]) * (1.0 + m4_ref[...]) + m3_ref[...]).astype(BF16)
    ff = jnp.zeros(x1.shape, F32)
    for c in range(D_FF // FF_CHUNK):
        cs = slice(c * FF_CHUNK, (c + 1) * FF_CHUNK)
        hid = jnp.square(jnp.maximum(_dot(hb, w1_ref[:, cs]), 0.0))
        ff = ff + _dot(hid.astype(BF16), w2_ref[cs, :])
    x2 = x1 + m5_ref[...] * ff
    y_ref[...] = _rms(x2, gfin_ref[...])


def _finish(x2, o_attn, o_ssm, ga, gs, mods, g_mlp, g_final, wts, *, nb, t, time_major):
    rows = x2.shape[0]
    if time_major:
        tm = nb
        grid = (1, t)
        row_blk = lambda i, j: j
        mod_spec = lambda k: pl.BlockSpec((nb, D_MODEL), lambda i, j: (0, k))
        os_spec = pl.BlockSpec((tm, SSM_WIDTH), lambda i, j: (j, 0))
    else:
        tm = min(TQ, t)
        nt = t // tm
        grid = (nb, nt)
        row_blk = lambda i, j: i * nt + j
        mod_spec = lambda k: pl.BlockSpec((None, 1, D_MODEL), lambda i, j: (i, 0, k))
        os_spec = pl.BlockSpec((tm, SSM_WIDTH), lambda i, j: (j, i))
    rspec = lambda w: pl.BlockSpec((tm, w), lambda i, j: (row_blk(i, j), 0))
    wspec = lambda a, b: pl.BlockSpec((a, b), lambda i, j: (0, 0), pipeline_mode=pl.Buffered(1))
    vec = pl.BlockSpec((1, D_MODEL), lambda i, j: (0, 0))
    return pl.pallas_call(
        _finish_kernel,
        out_shape=jax.ShapeDtypeStruct((rows, D_MODEL), F32),
        grid=grid,
        in_specs=[rspec(D_MODEL), rspec(ATTN_WIDTH), os_spec, rspec(D_MODEL), rspec(D_MODEL),
                  mod_spec(2), mod_spec(3), mod_spec(4), mod_spec(5), vec, vec,
                  wspec(ATTN_WIDTH, D_MODEL), wspec(SSM_WIDTH, D_MODEL), wspec(D_MODEL, D_MODEL),
                  wspec(D_MODEL, D_FF), wspec(D_FF, D_MODEL)],
        out_specs=rspec(D_MODEL),
        compiler_params=_cparams(("arbitrary", "arbitrary")),
        name="finish_layer",
    )(x2, o_attn, o_ssm, ga, gs, mods, mods, mods, mods, g_mlp.reshape(1, -1), g_final.reshape(1, -1), *wts)


def _start_pages(pt_ref, n_idx, n_pages, cache, buf, sem, slot, rows_per_page):
    def body(p, carry):
        phys = pt_ref[n_idx * n_pages + p]
        dst = buf.at[slot, pl.ds(pl.multiple_of(p * rows_per_page, rows_per_page), rows_per_page)]
        pltpu.make_async_copy(cache.at[phys], dst, sem.at[slot]).start()
        return carry
    lax.fori_loop(0, n_pages, body, 0)


def _wait_pages(n_pages, cache, buf, sem, slot, rows_per_page):
    def body(p, carry):
        dst = buf.at[slot, pl.ds(pl.multiple_of(p * rows_per_page, rows_per_page), rows_per_page)]
        pltpu.make_async_copy(cache.at[0], dst, sem.at[slot]).wait()
        return carry
    lax.fori_loop(0, n_pages, body, 0)


def _paged_prefetch(pt_ref, n_pages, caches, bufs, sems, rows_per_page):
    n = pl.program_id(0)
    slot = lax.rem(n, 2)

    @pl.when(n == 0)
    def _():
        for cache, buf, sem in zip(caches, bufs, sems):
            _start_pages(pt_ref, n, n_pages, cache, buf, sem, slot, rows_per_page)

    @pl.when(n + 1 < pl.num_programs(0))
    def _():
        for cache, buf, sem in zip(caches, bufs, sems):
            _start_pages(pt_ref, n + 1, n_pages, cache, buf, sem, 1 - slot, rows_per_page)

    for cache, buf, sem in zip(caches, bufs, sems):
        _wait_pages(n_pages, cache, buf, sem, slot, rows_per_page)
    return slot


def _cmp_sample_kernel(pt_ref, kcache, vcache, knew_ref, vnew_ref, q_ref, bias_ref,
                       w1_ref, pe_ref, b1_ref, w2_ref, ovl_ref, ocmp_ref, selq_ref,
                       kbuf, vbuf, ksem, vsem, *, n_pages, n_slc, q_pos0, dec_t):
    chunks_per_page = PAGE_SIZE // CMP_STRIDE
    slot = _paged_prefetch(pt_ref, n_pages, (kcache, vcache), (kbuf, vbuf), (ksem, vsem), chunks_per_page)

    def compress(buf, new_ref, i):
        w1 = w1_ref[i]
        new_p1 = _dot(new_ref[...].astype(BF16), w1)[0:1, KV_WIDTH:]
        return _compress_core(buf[slot].astype(BF16), new_p1, w1, pe_ref[i], b1_ref[i], w2_ref[i])

    kc = compress(kbuf, knew_ref, 0)
    vc = compress(vbuf, vnew_ref, 1)

    qb = q_ref[...].astype(BF16)
    bc = bias_ref[...]
    valid = bc > 0.5 * NEG_INF
    s = jnp.where(valid, _dot_t(qb, kc.astype(BF16)) + bc, NEG_INF)
    p = jnp.where(valid, jnp.exp(s - jnp.max(s, axis=-1, keepdims=True)), 0.0)
    p = p / jnp.maximum(jnp.sum(p, axis=-1, keepdims=True), TINY)
    ocmp_ref[...] = _dot(p.astype(BF16), vc.astype(BF16))
    gt = N_KV_HEADS * dec_t
    imp = p[0:gt]
    for r in range(1, Q_PER_KV):
        imp = imp + p[r * gt:(r + 1) * gt]
    score = _split3_dot(imp, ovl_ref[...])
    tt = lax.broadcasted_iota(jnp.int32, (gt, 1), 0)
    for g in range(1, N_KV_HEADS):
        tt = jnp.where(tt >= dec_t, tt - dec_t, tt)
    selq_ref[...] = _select_mask(score, q_pos0 + tt, n_slc)


def _cmp_sample(page_table, kcache, vcache, knew, vnew, q_rows, bias_cmp, cw, ovl, *, nb, n_pages, dec_t):
    w1, pe, b1, w2 = cw
    nc = n_pages * (PAGE_SIZE // CMP_STRIDE)
    past = n_pages * PAGE_SIZE
    n_slc = -(-(past + dec_t) // SLC_BLOCK)
    sel_w = ovl.shape[1]
    qr = q_rows.shape[1]
    gt = N_KV_HEADS * dec_t
    kern = functools.partial(_cmp_sample_kernel, n_pages=n_pages, n_slc=n_slc, q_pos0=past, dec_t=dec_t)
    full = lambda *s: pl.BlockSpec(s, lambda n, pt: (0,) * len(s))
    per_n = lambda a, b: pl.BlockSpec((None, a, b), lambda n, pt: (n, 0, 0))
    grid_spec = pltpu.PrefetchScalarGridSpec(
        num_scalar_prefetch=1,
        grid=(nb,),
        in_specs=[pl.BlockSpec(memory_space=pl.ANY), pl.BlockSpec(memory_space=pl.ANY),
                  per_n(SUBLANES, CHUNK_W), per_n(SUBLANES, CHUNK_W), per_n(qr, LANES),
                  full(qr, nc),
                  full(2, CHUNK_W, CMP_RATIO * KV_WIDTH), full(2, SUBLANES, CHUNK_W), full(2, 1, KV_WIDTH),
                  full(2, KV_WIDTH, KV_WIDTH), full(nc, sel_w)],
        out_specs=(per_n(qr, LANES), per_n(gt, sel_w)),
        scratch_shapes=[pltpu.VMEM((2, nc, CHUNK_W), F32), pltpu.VMEM((2, nc, CHUNK_W), F32),
                        pltpu.SemaphoreType.DMA((2,)), pltpu.SemaphoreType.DMA((2,))])
    return pl.pallas_call(
        kern,
        out_shape=(jax.ShapeDtypeStruct((nb, qr, LANES), F32), jax.ShapeDtypeStruct((nb, gt, sel_w), F32)),
        grid_spec=grid_spec,
        compiler_params=_cparams(("arbitrary",)),
        name="cmp_sample",
    )(page_table.reshape(-1), kcache, vcache, knew, vnew, q_rows, bias_cmp, w1, pe, b1, w2, ovl)


def _softmax_parts(parts):
    m = parts[0].max(axis=-1, keepdims=True)
    for s in parts[1:]:
        m = jnp.maximum(m, s.max(axis=-1, keepdims=True))
    es = [jnp.exp(s - m) for s in parts]
    l = es[0].sum(axis=-1, keepdims=True)
    for e in es[1:]:
        l = l + e.sum(axis=-1, keepdims=True)
    inv = 1.0 / jnp.maximum(l, TINY)
    return [e * inv for e in es]


def _slc_win_sample_kernel(pt_ref, kcache, vcache, ksn_ref, vsn_ref, kwn_ref, vwn_ref, kwin_ref, vwin_ref,
                           q_ref, selq_ref, ocmp_ref, gate_ref, bslc_ref, bnew_ref, bwin_ref, exp_ref,
                           o_ref, kwo_ref, vwo_ref, kbuf, vbuf, ksem, vsem, *, n_pages, dec_t):
    slot = _paged_prefetch(pt_ref, n_pages, (kcache, vcache), (kbuf, vbuf), (ksem, vsem), PAGE_SIZE)
    qb = q_ref[...].astype(BF16)
    w_buf = kwin_ref.shape[0]
    past_blk = n_pages * PAGE_SIZE // SLC_BLOCK

    selq = jnp.concatenate([selq_ref[...]] * Q_PER_KV, axis=0)
    mask_past = _dot(selq[:, 0:past_blk].astype(BF16), exp_ref[...])
    s_past = _dot_t(qb, kbuf[slot].astype(BF16)) + bslc_ref[...] + mask_past
    bnew = bnew_ref[...]
    s_new = _dot_t(qb, ksn_ref[...].astype(BF16)) + bnew + selq[:, past_blk:past_blk + 1]
    p_past, p_new = _softmax_parts([s_past, s_new])
    o_slc = _dot(p_past.astype(BF16), vbuf[slot].astype(BF16)) + _dot(p_new.astype(BF16), vsn_ref[...].astype(BF16))

    kwin = kwin_ref[...]
    vwin = vwin_ref[...]
    s_buf = _dot_t(qb, kwin.astype(BF16)) + bwin_ref[...]
    s_wn = _dot_t(qb, kwn_ref[...].astype(BF16)) + bnew
    p_buf, p_wn = _softmax_parts([s_buf, s_wn])
    o_win = _dot(p_buf.astype(BF16), vwin.astype(BF16)) + _dot(p_wn.astype(BF16), vwn_ref[...].astype(BF16))

    gate = gate_ref[...]
    o_ref[...] = gate[:, 0:1] * ocmp_ref[...] + gate[:, 1:2] * o_slc + gate[:, 2:3] * o_win

    row8 = lax.broadcasted_iota(jnp.int32, (SUBLANES, KV_WIDTH), 0)
    for old, new_ref, out_ref in ((kwin, kwn_ref, kwo_ref), (vwin, vwn_ref, vwo_ref)):
        shifted = pltpu.roll(old, w_buf - dec_t, 0)
        new8 = pltpu.roll(new_ref[0:SUBLANES, :], SUBLANES - dec_t, 0)
        out_ref[0:w_buf - SUBLANES, :] = shifted[0:w_buf - SUBLANES]
        out_ref[w_buf - SUBLANES:, :] = jnp.where(row8 >= SUBLANES - dec_t, new8, shifted[w_buf - SUBLANES:])


def _slc_win_sample(page_table, kcache, vcache, new_kv, kwin, vwin, q_rows, selq, o_cmp, gates,
                    bslc, bnew, bwin, expand, *, nb, n_pages, dec_t):
    past = n_pages * PAGE_SIZE
    w_buf = kwin.shape[1]
    qr = q_rows.shape[1]
    kern = functools.partial(_slc_win_sample_kernel, n_pages=n_pages, dec_t=dec_t)
    full = lambda *s: pl.BlockSpec(s, lambda n, pt: (0,) * len(s))
    per_n = lambda a, b: pl.BlockSpec((None, a, b), lambda n, pt: (n, 0, 0))
    new_spec = lambda i: pl.BlockSpec((None, None, LANES, KV_WIDTH), lambda n, pt: (i, n, 0, 0))
    grid_spec = pltpu.PrefetchScalarGridSpec(
        num_scalar_prefetch=1,
        grid=(nb,),
        in_specs=[pl.BlockSpec(memory_space=pl.ANY), pl.BlockSpec(memory_space=pl.ANY),
                  new_spec(0), new_spec(1), new_spec(2), new_spec(3),
                  per_n(w_buf, KV_WIDTH), per_n(w_buf, KV_WIDTH),
                  per_n(qr, LANES), per_n(selq.shape[1], selq.shape[2]), per_n(qr, LANES), per_n(qr, LANES),
                  full(qr, past), full(qr, LANES), full(qr, w_buf), full(*expand.shape)],
        out_specs=(per_n(qr, LANES), per_n(w_buf, KV_WIDTH), per_n(w_buf, KV_WIDTH)),
        scratch_shapes=[pltpu.VMEM((2, past, KV_WIDTH), F32), pltpu.VMEM((2, past, KV_WIDTH), F32),
                        pltpu.SemaphoreType.DMA((2,)), pltpu.SemaphoreType.DMA((2,))])
    return pl.pallas_call(
        kern,
        out_shape=(jax.ShapeDtypeStruct((nb, qr, LANES), F32),
                   jax.ShapeDtypeStruct((nb, w_buf, KV_WIDTH), F32),
                   jax.ShapeDtypeStruct((nb, w_buf, KV_WIDTH), F32)),
        grid_spec=grid_spec,
        compiler_params=_cparams(("arbitrary",)),
        name="slc_win_sample",
    )(page_table.reshape(-1), kcache, vcache, new_kv, new_kv, new_kv, new_kv, kwin, vwin,
      q_rows, selq, o_cmp, gates, bslc, bnew, bwin, expand)


def _prompt_dists(t):
    a = np.arange(TQ)[:, None]
    b = np.arange(TQ)[None, :]
    near = []
    for off in range(3):
        d = a - b + off * TQ
        ok = (d >= 0) & (d <= WINDOW if off == 2 else True)
        near.append(np.where(ok, d, -1))
    c = t // CMP_STRIDE
    n_cmp = c - CMP_RATIO + 1
    tq = np.arange(t)[:, None]
    cc = np.arange(LANES * (-(-c // LANES)))[None, :]
    dc = tq - (cc * CMP_STRIDE + CMP_BLOCK - 1)
    dc = np.where((dc >= 0) & (cc < n_cmp), dc, -1)
    return np.stack(near).astype(np.int32), dc[None].astype(np.int32)


def _sample_dists(past, dec_t, w_buf):
    tq = np.arange(dec_t)[:, None]
    q_pos = past + tq
    n_cmp = past // CMP_STRIDE
    d_cmp = q_pos - (np.arange(n_cmp)[None, :] * CMP_STRIDE + CMP_BLOCK - 1)
    d_cmp = np.where(d_cmp >= 0, d_cmp, -1)
    d_slc = q_pos - np.arange(past)[None, :]
    tn = np.arange(LANES)[None, :]
    d_new = np.where((tn < dec_t) & (tq - tn >= 0), tq - tn, -1)
    d_win = q_pos - (past - w_buf + np.arange(w_buf)[None, :])
    d_win = np.where((d_win >= 0) & (d_win <= WINDOW), d_win, -1)
    segs = [d_cmp, d_slc, d_new, d_win]
    bounds = np.cumsum([0] + [s.shape[1] for s in segs])
    return np.concatenate(segs, axis=1)[:, None, :].astype(np.int32), bounds


def _kv_5d(a, nb, t):
    return a.reshape(1, nb, t, N_KV_HEADS, HEAD_DIM)


def kernel(x_prompt, x_sample, cache_k_cmp, cache_v_cmp, cache_k_slc, cache_v_slc, cache_k_win, cache_v_win, state_ssm_re, state_ssm_im, page_table, c_prompt, c_sample, rel_bias, w_ada, b_ada, g_norm_mix, g_norm_mlp, w_in, cmp_pe, cmp_w1, cmp_b1, cmp_w2, ssm_lambda_re, ssm_lambda_im, ssm_log_dt, ssm_b_re, ssm_b_im, ssm_c_re, ssm_c_im, ssm_d, w_glu, w_br_attn, w_br_ssm, w_out, w_ff1, w_ff2, g_final):
    assert w_ada.shape[0] == 1, "single layer"
    nb, t, _ = x_prompt.shape
    ns, dec_t, _ = x_sample.shape
    n_pages = page_table.shape[1]
    past = n_pages * PAGE_SIZE
    w_buf = cache_k_win.shape[2]
    n_phys = cache_k_cmp.shape[1]
    assert t % TQ == 0 and t // SLC_BLOCK <= LANES and dec_t <= SUBLANES and w_buf >= 2 * SUBLANES
    assert SLC_BLOCK == 1 << SLC_SHIFT
    assert ns % SUBLANES == 0 and nb % SUBLANES == 0

    w_perm = _permute_w_in(w_in[0])
    cw = _compress_weights(cmp_pe[0], cmp_w1[0], cmp_b1[0], cmp_w2[0])
    bblk = _ssm_weights(ssm_b_re[0], ssm_b_im[0], ssm_c_re[0], ssm_c_im[0])
    w_glu_b = w_glu[0].astype(BF16)
    fin_w = (w_br_attn[0].astype(BF16), w_br_ssm[0].astype(BF16), w_out[0].astype(BF16),
             w_ff1[0].astype(BF16), w_ff2[0].astype(BF16))

    mods = _ada_mods(jnp.concatenate([c_prompt, c_sample], axis=0), w_ada[0], b_ada[0])
    mods_p = mods[:nb].reshape(nb, 1, 6 * D_MODEL)
    mods_s = mods[nb:]

    near_d, cmp_d = _prompt_dists(t)
    bt = _bias_from_dist(rel_bias, jnp.asarray(near_d))
    bcmp = _bias_from_dist(rel_bias, jnp.asarray(cmp_d))[:, 0]
    samp_d, sb = _sample_dists(past, dec_t, w_buf)
    bs = _bias_from_dist(rel_bias, jnp.asarray(samp_d))[:, :, 0]
    bs = bs.reshape(N_KV_HEADS, Q_PER_KV, dec_t, -1).transpose(1, 0, 2, 3).reshape(N_Q_HEADS * dec_t, -1)
    b_cmp_s, b_slc_s, b_new_s, b_win_s = (bs[:, sb[i]:sb[i + 1]] for i in range(4))

    xp = x_prompt.reshape(nb * t, D_MODEL)
    q_p, kv_p, u_p, ga_p, gs_p, gn_p = _inproj(xp, mods_p, g_norm_mix[0], w_perm, nb=nb, t=t, time_major=False)
    kv_p = kv_p.reshape(6, nb, t, KV_WIDTH)
    c_p = t // CMP_STRIDE
    kcvc = _compress_prompt(kv_p[0:2].reshape(2, nb, c_p, CHUNK_W), *cw)
    n_slc_p = t // SLC_BLOCK
    ovl_p = _overlap_matrix(c_p - CMP_RATIO + 1, n_slc_p, c_p, LANES)
    o_attn_p = _attn_prompt(rel_bias, q_p, kcvc, bcmp, kv_p, bt, gn_p, ovl_p, nb=nb, t=t)
    zeros = jnp.zeros((nb, SSM_CH), F32)
    o_ssm_p, p_re, p_im = _ssm(u_p.reshape(t * nb, SSM_WIDTH), zeros, zeros, ssm_lambda_re[0], ssm_lambda_im[0],
                               ssm_log_dt[0], bblk, ssm_d[0], w_glu_b, nb=nb, t=t)
    y_p = _finish(xp, o_attn_p, o_ssm_p.reshape(t, nb * SSM_WIDTH), ga_p, gs_p, mods_p, g_norm_mlp[0], g_final,
                  fin_w, nb=nb, t=t, time_major=False)

    xs = x_sample.transpose(1, 0, 2).reshape(dec_t * ns, D_MODEL)
    q_s, kv_s, u_s, ga_s, gs_s, gn_s = _inproj(xs, mods_s, g_norm_mix[0], w_perm, nb=ns, t=dec_t, time_major=True)
    gt = N_KV_HEADS * dec_t
    qr = N_Q_HEADS * dec_t
    q_rows = q_s.reshape(N_KV_HEADS, Q_PER_KV, dec_t, ns, LANES).transpose(3, 1, 0, 2, 4).reshape(ns, qr, LANES)
    kv_s4 = kv_s.reshape(6, dec_t, ns, KV_WIDTH).transpose(0, 2, 1, 3)
    new_chunk = jnp.pad(kv_s4[0:2].reshape(2, ns, 1, dec_t * KV_WIDTH),
                        ((0, 0), (0, 0), (0, SUBLANES - 1), (0, CHUNK_W - dec_t * KV_WIDTH)))
    new_rows = jnp.pad(kv_s4[2:6], ((0, 0), (0, 0), (0, LANES - dec_t), (0, 0)))
    n_slc_s = -(-(past + dec_t) // SLC_BLOCK)
    sel_w = LANES * (-(-n_slc_s // LANES))
    ovl_s = _overlap_matrix(past // CMP_STRIDE, n_slc_s, past // CMP_STRIDE, sel_w)
    chunks_per_page = PAGE_SIZE // CMP_STRIDE
    o_cmp_s, selq_s = _cmp_sample(page_table, cache_k_cmp.reshape(n_phys, chunks_per_page, CHUNK_W),
                                  cache_v_cmp.reshape(n_phys, chunks_per_page, CHUNK_W),
                                  new_chunk[0], new_chunk[1], q_rows, b_cmp_s, cw, ovl_s,
                                  nb=ns, n_pages=n_pages, dec_t=dec_t)
    gates = gn_s[:, :3 * N_Q_HEADS].reshape(dec_t, ns, 3, N_KV_HEADS, Q_PER_KV).transpose(1, 4, 3, 0, 2)
    gates = jnp.pad(gates.reshape(ns, qr, 3), ((0, 0), (0, 0), (0, LANES - 3)))
    past_blk = past // SLC_BLOCK
    expand = jnp.asarray(np.arange(past_blk)[:, None] == (np.arange(past)[None, :] // SLC_BLOCK), BF16)
    o_s, s_kw, s_vw = _slc_win_sample(
        page_table, cache_k_slc.reshape(n_phys, PAGE_SIZE, KV_WIDTH), cache_v_slc.reshape(n_phys, PAGE_SIZE, KV_WIDTH),
        new_rows, cache_k_win.reshape(ns, w_buf, KV_WIDTH), cache_v_win.reshape(ns, w_buf, KV_WIDTH),
        q_rows, selq_s, o_cmp_s, gates, b_slc_s, b_new_s, b_win_s, expand, nb=ns, n_pages=n_pages, dec_t=dec_t)
    o6 = o_s.reshape(ns, Q_PER_KV, N_KV_HEADS, dec_t, N_KV_HEADS, HEAD_DIM)
    o_sel = jnp.stack([o6[:, :, g, :, g, :] for g in range(N_KV_HEADS)], axis=2)
    o_attn_s = o_sel.transpose(3, 0, 2, 1, 4).reshape(dec_t * ns, ATTN_WIDTH)
    o_ssm_s, s_re, s_im = _ssm(u_s, state_ssm_re[0].reshape(ns, SSM_CH), state_ssm_im[0].reshape(ns, SSM_CH),
                               ssm_lambda_re[0], ssm_lambda_im[0], ssm_log_dt[0], bblk, ssm_d[0], w_glu_b,
                               nb=ns, t=dec_t)
    y_s = _finish(xs, o_attn_s, o_ssm_s, ga_s, gs_s, mods_s, g_norm_mlp[0], g_final, fin_w,
                  nb=ns, t=dec_t, time_major=True)

    keep = min(WINDOW, t)
    ssm_shape = (1, -1, N_SSM_GROUPS, SSM_STATE)
    s_new = kv_s4.reshape(6, 1, ns, dec_t, N_KV_HEADS, HEAD_DIM)
    return (y_p.reshape(nb, t, D_MODEL), y_s.reshape(dec_t, ns, D_MODEL).transpose(1, 0, 2),
            _kv_5d(kv_p[0], nb, t), _kv_5d(kv_p[1], nb, t), _kv_5d(kv_p[2], nb, t), _kv_5d(kv_p[3], nb, t),
            _kv_5d(kv_p[4][:, t - keep:], nb, keep), _kv_5d(kv_p[5][:, t - keep:], nb, keep),
            p_re.reshape(ssm_shape), p_im.reshape(ssm_shape),
            s_new[0], s_new[1], s_new[2], s_new[3],
            _kv_5d(s_kw, ns, w_buf), _kv_5d(s_vw, ns, w_buf),
            s_re.reshape(ssm_shape), s_im.reshape(ssm_shape))
```
